```python
import math
import jax, jax.numpy as jnp
from jax import lax
import numpy as np

D_MODEL = 1024
BATCH = 1
SEQ = 16384
DEPTH = 2
DEC_BATCH = 128
DEC_SEQ = 4
PAST_LEN = 16384
PAGE_SIZE = 128

MLA_HEADS = 16
MLA_Q_LORA = 512
MLA_KV_LORA = 256
MLA_NOPE = 64
MLA_ROPE = 32
MLA_VDIM = 64
ROPE_THETA = 10000.0
NSA_HEADS = 16
NSA_KV_HEADS = 2
NSA_GROUP = NSA_HEADS // NSA_KV_HEADS
NSA_HDIM = 64
N_BRANCH = 3
CMP_LEN = 32
CMP_STRIDE = 16
SEL_BLOCK = 64
SEL_TOPK = 16
WINDOW = 512
SEL_OVERLAP_W = (1.0, 2.0, 2.0, 2.0, 1.0)
REL_BUCKETS = 32
REL_MAX_DIST = 128
N_EXPERTS = 256
TOP_K = 8
N_EXPERT_GROUPS = 8
TOPK_GROUPS = 4
D_EXPERT = 256
D_SHARED = 256
ROUTED_SCALE = 2.5
MOE_BLOCK = 32
QUERY_BLOCK = 128
N_MIXERS = 2
N_MLA_LAYERS = (DEPTH + 1) // 2
N_NSA_LAYERS = DEPTH // 2
DEEPNORM_ALPHA = (2 * DEPTH) ** 0.25
DEEPNORM_BETA = (8 * DEPTH) ** -0.25
LN_EPS = 1e-5
RMS_EPS = 1e-6
NEG_INF = -1e30
SEL_BIG = 1e9

kernel_name = 'hybrid_mla_nsa_moe_deepnorm_adaln_step'


def layer_norm(x, g, b):
    xf = x.astype(jnp.float32)
    mu = xf.mean(-1, keepdims=True)
    var = jnp.square(xf - mu).mean(-1, keepdims=True)
    return ((xf - mu) * lax.rsqrt(var + LN_EPS) * g + b).astype(x.dtype)


def rms_norm(x, g):
    xf = x.astype(jnp.float32)
    return (xf * lax.rsqrt(jnp.square(xf).mean(-1, keepdims=True) + RMS_EPS) * g).astype(x.dtype)


def rope(x, pos):
    half = x.shape[-1] // 2
    inv = ROPE_THETA ** (-jnp.arange(half, dtype=jnp.float32) / half)
    ang = pos.astype(jnp.float32)[..., None] * inv
    cos, sin = jnp.cos(ang), jnp.sin(ang)
    xf = x.astype(jnp.float32)
    x1, x2 = xf[..., :half], xf[..., half:]
    return jnp.concatenate([x1 * cos - x2 * sin, x2 * cos + x1 * sin], -1).astype(x.dtype)


def masked_softmax(s, mask):
    s = jnp.where(mask, s.astype(jnp.float32), NEG_INF)
    return jax.nn.softmax(s, axis=-1) * mask


def t5_bucket(dist):
    max_exact = REL_BUCKETS // 2
    n = jnp.maximum(dist, 0)
    nf = jnp.maximum(n, max_exact).astype(jnp.float32)
    large = max_exact + (jnp.log(nf / max_exact) / math.log(REL_MAX_DIST / max_exact)
                         * (REL_BUCKETS - max_exact)).astype(jnp.int32)
    return jnp.where(n < max_exact, n, jnp.minimum(large, REL_BUCKETS - 1))


def swiglu(x, wg, wu, wd):
    return (jax.nn.silu(x @ wg) * (x @ wu)) @ wd


def adaln(c, w, b):
    m = jax.nn.silu(c) @ w + b
    return jnp.split(m[:, None, :], 6, axis=-1)


def mla_project(h, pos, w_in, g_q, g_kv, w_uq):
    z = h @ w_in
    cq = rms_norm(z[..., :MLA_Q_LORA], g_q)
    ckv = rms_norm(z[..., MLA_Q_LORA:MLA_Q_LORA + MLA_KV_LORA], g_kv)
    k_rope = rope(z[..., MLA_Q_LORA + MLA_KV_LORA:], pos)
    q = jnp.einsum('bsc,chd->bshd', cq, w_uq)
    q_rope = rope(q[..., MLA_NOPE:], pos[:, None])
    return q[..., :MLA_NOPE], q_rope, ckv, k_rope


def mla_prompt(h, w_in, g_q, g_kv, w_uq, w_uk, w_uv, w_o):
    b, s, _ = h.shape
    pos = jnp.arange(s)
    q_nope, q_rope, ckv, k_rope = mla_project(h, pos, w_in, g_q, g_kv, w_uq)
    k_nope = jnp.einsum('bsc,chd->bshd', ckv, w_uk)
    v = jnp.einsum('bsc,chd->bshd', ckv, w_uv)
    scale = (MLA_NOPE + MLA_ROPE) ** -0.5

    def block(i):
        s0 = i * QUERY_BLOCK
        qn = lax.dynamic_slice_in_dim(q_nope, s0, QUERY_BLOCK, 1)
        qr = lax.dynamic_slice_in_dim(q_rope, s0, QUERY_BLOCK, 1)
        sc = (jnp.einsum('bqhd,bkhd->bhqk', qn, k_nope)
              + jnp.einsum('bqhr,bkr->bhqk', qr, k_rope)) * scale
        mask = pos[None, :] <= (s0 + jnp.arange(QUERY_BLOCK))[:, None]
        p = masked_softmax(sc, mask).astype(v.dtype)
        return jnp.einsum('bhqk,bkhd->bqhd', p, v)

    o = lax.map(block, jnp.arange(s // QUERY_BLOCK))
    o = o.transpose(1, 0, 2, 3, 4).reshape(b, s, MLA_HEADS, MLA_VDIM)
    y = jnp.einsum('bshd,hde->bse', o, w_o)
    return y, jnp.concatenate([ckv, k_rope], -1)


def mla_sample(h, cache, page_table, w_in, g_q, g_kv, w_uq, w_uk, w_uv, w_o):
    _, ds, _ = h.shape
    past = page_table.shape[1] * PAGE_SIZE
    qpos = past + jnp.arange(ds)
    q_nope, q_rope, ckv, k_rope = mla_project(h, qpos, w_in, g_q, g_kv, w_uq)
    q_lat = jnp.einsum('bqhd,chd->bqhc', q_nope, w_uk)
    new_rows = jnp.concatenate([ckv, k_rope], -1)
    mask = jnp.arange(past + ds)[None, :] <= qpos[:, None]
    scale = (MLA_NOPE + MLA_ROPE) ** -0.5

    def one(args):
        ql, qr, rows, pt = args
        keys = jnp.concatenate([cache[pt].reshape(past, -1), rows], 0)
        c, kr = keys[:, :MLA_KV_LORA], keys[:, MLA_KV_LORA:]
        sc = (jnp.einsum('qhc,kc->hqk', ql, c) + jnp.einsum('qhr,kr->hqk', qr, kr)) * scale
        p = masked_softmax(sc, mask).astype(c.dtype)
        return jnp.einsum('hqk,kc->qhc', p, c)

    o_lat = lax.map(one, (q_lat, q_rope, new_rows, page_table))
    o = jnp.einsum('bqhc,chd->bqhd', o_lat, w_uv)
    y = jnp.einsum('bqhd,hde->bqe', o, w_o)
    return y, new_rows


def nsa_project(h, w_in):
    b, s, _ = h.shape
    nq = NSA_HEADS * NSA_HDIM
    nkv = N_BRANCH * 2 * NSA_KV_HEADS * NSA_HDIM
    z = h @ w_in
    q = z[..., :nq].reshape(b, s, NSA_HEADS, NSA_HDIM)
    kv = z[..., nq:nq + nkv].reshape(b, s, N_BRANCH, 2, NSA_KV_HEADS, NSA_HDIM)
    gate = jax.nn.sigmoid(z[..., nq + nkv:].reshape(b, s, NSA_HEADS, N_BRANCH))
    return q, kv[:, :, 0], kv[:, :, 1], kv[:, :, 2], gate


def nsa_compress(kv, pe_k, w_ck, pe_v, w_cv):
    *lead, t, _, g, d = kv.shape
    ch = kv.reshape(*lead, t // CMP_STRIDE, CMP_STRIDE, 2, g, d)
    blocks = jnp.concatenate([ch[..., :-1, :, :, :, :], ch[..., 1:, :, :, :, :]], axis=-4)
    kc = jnp.einsum('...nlgd,lde->...nge', blocks[..., 0, :, :] + pe_k[:, None, :], w_ck)
    vc = jnp.einsum('...nlgd,lde->...nge', blocks[..., 1, :, :] + pe_v[:, None, :], w_cv)
    return kc, vc


def cmp_to_sel_scores(p, n_sel):
    r = SEL_BLOCK // CMP_STRIDE
    pp = jnp.pad(p, [(0, 0)] * (p.ndim - 1) + [(1, 1)])
    a = pp[..., :r * n_sel].reshape(*p.shape[:-1], n_sel, r)
    w = jnp.array(SEL_OVERLAP_W, dtype=p.dtype)
    return a @ w[:r] + w[r] * pp[..., r::r]


def nsa_attend(q, qpos, gate, kc, vc, c_end, kvs_blk, kvw, wpos, rel_bias):
    nq = q.shape[0]
    g, r, d = NSA_KV_HEADS, NSA_GROUP, NSA_HDIM
    scale = d ** -0.5
    qg = q.reshape(nq, g, r, d)
    tab = rel_bias.reshape(REL_BUCKETS, g, r)

    def bias2(dist):
        return tab[t5_bucket(dist)].transpose(2, 3, 0, 1)

    dc = qpos[:, None] - c_end[None, :]
    s1 = jnp.einsum('qgrd,ngd->grqn', qg, kc) * scale + bias2(dc)
    p1 = masked_softmax(s1, dc >= 0)
    o_cmp = jnp.einsum('grqn,ngd->qgrd', p1.astype(vc.dtype), vc)

    n_blk = kvs_blk.shape[0]
    k_sel = min(SEL_TOPK, n_blk)
    imp = cmp_to_sel_scores(p1.sum(1), n_blk)
    blk = jnp.arange(n_blk)
    cur = qpos // SEL_BLOCK
    forced = (blk == 0) | (blk == cur[:, None]) | (blk == cur[:, None] - 1)
    valid = blk * SEL_BLOCK <= qpos[:, None]
    imp = jnp.where(forced, SEL_BIG, jnp.where(valid, imp, -SEL_BIG))
    _, sel = lax.top_k(imp, k_sel)
    kvt = kvs_blk.transpose(3, 0, 1, 2, 4)
    g_idx = jnp.arange(g)[:, None, None]
    kvsel = kvt[g_idx, sel]
    tok = sel[..., None] * SEL_BLOCK + jnp.arange(SEL_BLOCK)
    ds_ = qpos[None, :, None, None] - tok
    b2 = tab[t5_bucket(ds_), g_idx[..., None]].transpose(0, 4, 1, 2, 3)
    s2 = jnp.einsum('qgrd,gqksd->grqks', qg, kvsel[..., 0, :]) * scale + b2
    p2 = masked_softmax(s2.reshape(g, r, nq, -1), (ds_ >= 0).reshape(g, 1, nq, -1))
    o_slc = jnp.einsum('grqm,gqmd->qgrd', p2.astype(kvsel.dtype), kvsel[..., 1, :].reshape(g, nq, -1, d))

    dw = qpos[:, None] - wpos[None, :]
    mask3 = (dw >= 0) & (dw < WINDOW) & (wpos[None, :] >= 0)
    s3 = jnp.einsum('qgrd,wgd->grqw', qg, kvw[:, 0]) * scale + bias2(dw)
    p3 = masked_softmax(s3, mask3)
    o_win = jnp.einsum('grqw,wgd->qgrd', p3.astype(kvw.dtype), kvw[:, 1])

    hd = (nq, NSA_HEADS, d)
    return (o_cmp.reshape(hd) * gate[..., 0:1] + o_slc.reshape(hd) * gate[..., 1:2]
            + o_win.reshape(hd) * gate[..., 2:3])


def nsa_prompt(h, w_in, pe_k, w_ck, pe_v, w_cv, w_o, rel_bias):
    b, s, _ = h.shape
    q, kv_c, kv_s, kv_w, gate = nsa_project(h, w_in)
    kc, vc = nsa_compress(kv_c, pe_k, w_ck, pe_v, w_cv)
    c_end = jnp.arange(kc.shape[1]) * CMP_STRIDE + CMP_LEN - 1
    kvs_blk = kv_s.reshape(b, s // SEL_BLOCK, SEL_BLOCK, 2, NSA_KV_HEADS, NSA_HDIM)
    kvw_pad = jnp.pad(kv_w, ((0, 0), (WINDOW, 0), (0, 0), (0, 0), (0, 0)))
    attend = jax.vmap(nsa_attend, in_axes=(0, None, 0, 0, 0, None, 0, 0, None, None))

    def block(i):
        s0 = i * QUERY_BLOCK
        qb = lax.dynamic_slice_in_dim(q, s0, QUERY_BLOCK, 1)
        gb = lax.dynamic_slice_in_dim(gate, s0, QUERY_BLOCK, 1)
        kvwb = lax.dynamic_slice_in_dim(kvw_pad, s0, WINDOW + QUERY_BLOCK, 1)
        qpos = s0 + jnp.arange(QUERY_BLOCK)
        wpos = s0 - WINDOW + jnp.arange(WINDOW + QUERY_BLOCK)
        return attend(qb, qpos, gb, kc, vc, c_end, kvs_blk, kvwb, wpos, rel_bias)

    o = lax.map(block, jnp.arange(s // QUERY_BLOCK))
    o = o.transpose(1, 0, 2, 3, 4).reshape(b, s, NSA_HEADS * NSA_HDIM)
    n_keep = min(WINDOW, s)
    return o @ w_o, kv_c, kv_s, kv_w[:, s - n_keep:]


def nsa_sample(h, cache_cmp, cache_slc, win_state, page_table, w_in, pe_k, w_ck, pe_v, w_cv, w_o, rel_bias):
    db, ds, _ = h.shape
    past = page_table.shape[1] * PAGE_SIZE
    total = past + ds
    t_pad = -(-total // SEL_BLOCK) * SEL_BLOCK
    q, kv_c, kv_s, kv_w, gate = nsa_project(h, w_in)
    w_buf = win_state.shape[1]
    win_cat = jnp.concatenate([win_state, kv_w], 1)
    qpos = past + jnp.arange(ds)
    wpos = past - w_buf + jnp.arange(w_buf + ds)
    c_end = jnp.arange(t_pad // CMP_STRIDE - 1) * CMP_STRIDE + CMP_LEN - 1
    row_shape = (past, 2, NSA_KV_HEADS, NSA_HDIM)

    def full_rows(cache, pt, new):
        rows = jnp.concatenate([cache[pt].reshape(row_shape), new], 0)
        return jnp.pad(rows, ((0, t_pad - total), (0, 0), (0, 0), (0, 0)))

    def one(args):
        qi, gi, new_c, new_s, kvw, pt = args
        kc, vc = nsa_compress(full_rows(cache_cmp, pt, new_c), pe_k, w_ck, pe_v, w_cv)
        kvs_blk = full_rows(cache_slc, pt, new_s).reshape(t_pad // SEL_BLOCK, SEL_BLOCK, 2, NSA_KV_HEADS, NSA_HDIM)
        return nsa_attend(qi, qpos, gi, kc, vc, c_end, kvs_blk, kvw, wpos, rel_bias)

    o = lax.map(one, (q, gate, kv_c, kv_s, win_cat, page_table))
    return o.reshape(db, ds, -1) @ w_o, kv_c, kv_s, win_cat[:, -w_buf:]


def routed_experts(x, top_e, w, w_gate, w_up, w_down):
    t, d = x.shape
    n_as = t * TOP_K
    e_flat = top_e.reshape(-1)
    tok = jnp.arange(n_as, dtype=jnp.int32) // TOP_K
    order = jnp.argsort(e_flat)
    e_s, tok_s, w_s = e_flat[order], tok[order], w.reshape(-1)[order]
    counts = jnp.bincount(e_flat, length=N_EXPERTS)
    starts = jnp.cumsum(counts) - counts
    padded = (counts + MOE_BLOCK - 1) // MOE_BLOCK * MOE_BLOCK
    pend = jnp.cumsum(padded)
    dest = (pend - padded)[e_s] + jnp.arange(n_as) - starts[e_s]
    n_blk = -(-n_as // MOE_BLOCK) + N_EXPERTS
    row_tok = jnp.full((n_blk * MOE_BLOCK,), t, jnp.int32).at[dest].set(tok_s)
    row_w = jnp.zeros((n_blk * MOE_BLOCK,), x.dtype).at[dest].set(w_s)
    blk_e = jnp.minimum(jnp.searchsorted(pend, jnp.arange(n_blk) * MOE_BLOCK, side='right'), N_EXPERTS - 1)
    xb = jnp.concatenate([x, jnp.zeros((1, d), x.dtype)])[row_tok].reshape(n_blk, MOE_BLOCK, d)

    def expert_block(args):
        xe, e = args
        return swiglu(xe, w_gate[e], w_up[e], w_down[e])

    yb = lax.map(expert_block, (xb, blk_e)).reshape(-1, d)
    return jax.ops.segment_sum(yb * row_w[:, None], row_tok, num_segments=t + 1)[:t]


def moe_ffn(h, w_router, b_router, w_gate, w_up, w_down, ws_gate, ws_up, ws_down):
    shape = h.shape
    x = h.reshape(-1, shape[-1])
    t = x.shape[0]
    per = N_EXPERTS // N_EXPERT_GROUPS
    scores = jax.nn.sigmoid((x @ w_router).astype(jnp.float32))
    biased = scores + b_router.astype(jnp.float32)
    grp = lax.top_k(biased.reshape(t, N_EXPERT_GROUPS, per), 2)[0].sum(-1)
    _, top_g = lax.top_k(grp, TOPK_GROUPS)
    gmask = jax.nn.one_hot(top_g, N_EXPERT_GROUPS, dtype=jnp.float32).sum(1) > 0
    biased = jnp.where(jnp.repeat(gmask, per, axis=1), biased, NEG_INF)
    _, top_e = lax.top_k(biased, TOP_K)
    w = jnp.take_along_axis(scores, top_e, 1)
    w = (w / w.sum(-1, keepdims=True) * ROUTED_SCALE).astype(x.dtype)
    routed = routed_experts(x, top_e, w, w_gate, w_up, w_down)
    return (routed + swiglu(x, ws_gate, ws_up, ws_down)).reshape(shape)


def setup_inputs(seed: int = 0) -> dict:
    key = jax.random.key(seed)
    keys = iter(jax.random.split(key, 48))

    def nrm(shape, scale):
        return jax.random.normal(next(keys), shape, jnp.float32) * scale

    n_pages = PAST_LEN // PAGE_SIZE
    n_used = DEC_BATCH * n_pages
    n_pool = n_used + n_used // 4
    page_table = jax.random.permutation(next(keys), n_pool)[:n_used].reshape(DEC_BATCH, n_pages).astype(jnp.int32)
    w_buf = min(WINDOW, PAST_LEN)
    g, d = NSA_KV_HEADS, NSA_HDIM
    nsa_in = NSA_HEADS * d + N_BRANCH * 2 * g * d + N_BRANCH * NSA_HEADS
    return {
        'x_prompt': nrm((BATCH, SEQ, D_MODEL), 1.0),
        'x_sample': nrm((DEC_BATCH, DEC_SEQ, D_MODEL), 1.0),
        'cache_mla': nrm((N_MLA_LAYERS, n_pool, PAGE_SIZE, MLA_KV_LORA + MLA_ROPE), 1.0),
        'cache_nsa_cmp': nrm((N_NSA_LAYERS, n_pool, PAGE_SIZE, 2, g, d), 1.0),
        'cache_nsa_slc': nrm((N_NSA_LAYERS, n_pool, PAGE_SIZE, 2, g, d), 1.0),
        'state_nsa_win': nrm((N_NSA_LAYERS, DEC_BATCH, w_buf, 2, g, d), 1.0),
        'page_table': page_table,
        'c_prompt': nrm((BATCH, D_MODEL), 1.0),
        'c_sample': nrm((DEC_BATCH, D_MODEL), 1.0),
        'mla_w_in': nrm((N_MLA_LAYERS, D_MODEL, MLA_Q_LORA + MLA_KV_LORA + MLA_ROPE), D_MODEL ** -0.5),
        'mla_g_q': 1.0 + nrm((N_MLA_LAYERS, MLA_Q_LORA), 0.02),
        'mla_g_kv': 1.0 + nrm((N_MLA_LAYERS, MLA_KV_LORA), 0.02),
        'mla_w_uq': nrm((N_MLA_LAYERS, MLA_Q_LORA, MLA_HEADS, MLA_NOPE + MLA_ROPE), MLA_Q_LORA ** -0.5),
        'mla_w_uk': nrm((N_MLA_LAYERS, MLA_KV_LORA, MLA_HEADS, MLA_NOPE), MLA_KV_LORA ** -0.5),
        'mla_w_uv': nrm((N_MLA_LAYERS, MLA_KV_LORA, MLA_HEADS, MLA_VDIM), MLA_KV_LORA ** -0.5),
        'mla_w_o': nrm((N_MLA_LAYERS, MLA_HEADS, MLA_VDIM, D_MODEL), (MLA_HEADS * MLA_VDIM) ** -0.5 * DEEPNORM_BETA),
        'nsa_w_in': nrm((N_NSA_LAYERS, D_MODEL, nsa_in), D_MODEL ** -0.5),
        'nsa_pe_k': nrm((N_NSA_LAYERS, CMP_LEN, d), 0.1),
        'nsa_w_ck': nrm((N_NSA_LAYERS, CMP_LEN, d, d), (CMP_LEN * d) ** -0.5),
        'nsa_pe_v': nrm((N_NSA_LAYERS, CMP_LEN, d), 0.1),
        'nsa_w_cv': nrm((N_NSA_LAYERS, CMP_LEN, d, d), (CMP_LEN * d) ** -0.5),
        'nsa_w_o': nrm((N_NSA_LAYERS, NSA_HEADS * d, D_MODEL), (NSA_HEADS * d) ** -0.5 * DEEPNORM_BETA),
        'rel_bias': nrm((REL_BUCKETS, NSA_HEADS), 0.3),
        'ada_w': nrm((DEPTH, D_MODEL, 6 * D_MODEL), 0.1 * D_MODEL ** -0.5),
        'ada_b': nrm((DEPTH, 6 * D_MODEL), 0.02),
        'ln_g': 1.0 + nrm((DEPTH, 2, D_MODEL), 0.02),
        'ln_b': nrm((DEPTH, 2, D_MODEL), 0.02),
        'moe_w_router': nrm((DEPTH, D_MODEL, N_EXPERTS), D_MODEL ** -0.5),
        'moe_b_router': nrm((DEPTH, N_EXPERTS), 0.01),
        'moe_w_gate': nrm((DEPTH, N_EXPERTS, D_MODEL, D_EXPERT), D_MODEL ** -0.5),
        'moe_w_up': nrm((DEPTH, N_EXPERTS, D_MODEL, D_EXPERT), D_MODEL ** -0.5),
        'moe_w_down': nrm((DEPTH, N_EXPERTS, D_EXPERT, D_MODEL), D_EXPERT ** -0.5 * DEEPNORM_BETA),
        'moe_ws_gate': nrm((DEPTH, D_MODEL, D_SHARED), D_MODEL ** -0.5),
        'moe_ws_up': nrm((DEPTH, D_MODEL, D_SHARED), D_MODEL ** -0.5),
        'moe_ws_down': nrm((DEPTH, D_SHARED, D_MODEL), D_SHARED ** -0.5 * DEEPNORM_BETA),
    }


def reference(x_prompt, x_sample, cache_mla, cache_nsa_cmp, cache_nsa_slc, state_nsa_win, page_table,
              c_prompt, c_sample, mla_w_in, mla_g_q, mla_g_kv, mla_w_uq, mla_w_uk, mla_w_uv, mla_w_o,
              nsa_w_in, nsa_pe_k, nsa_w_ck, nsa_pe_v, nsa_w_cv, nsa_w_o, rel_bias, ada_w, ada_b, ln_g, ln_b,
              moe_w_router, moe_b_router, moe_w_gate, moe_w_up, moe_w_down, moe_ws_gate, moe_ws_up, moe_ws_down):
    xp, xs = x_prompt, x_sample
    mla_p, mla_s, cmp_p, cmp_s, slc_p, slc_s, win_p, win_s = [], [], [], [], [], [], [], []
    for l in range(DEPTH):
        j = l // N_MIXERS
        mp_ = adaln(c_prompt, ada_w[l], ada_b[l])
        ms_ = adaln(c_sample, ada_w[l], ada_b[l])
        hp = xp * (1.0 + mp_[1]) + mp_[0]
        hs = xs * (1.0 + ms_[1]) + ms_[0]
        if l % N_MIXERS == 0:
            mla_w = (mla_w_in[j], mla_g_q[j], mla_g_kv[j], mla_w_uq[j], mla_w_uk[j], mla_w_uv[j], mla_w_o[j])
            yp, rows_p = mla_prompt(hp, *mla_w)
            ys, rows_s = mla_sample(hs, cache_mla[j], page_table, *mla_w)
            mla_p.append(rows_p)
            mla_s.append(rows_s)
        else:
            nsa_w = (nsa_w_in[j], nsa_pe_k[j], nsa_w_ck[j], nsa_pe_v[j], nsa_w_cv[j], nsa_w_o[j], rel_bias)
            yp, cp, sp, wp = nsa_prompt(hp, *nsa_w)
            ys, cs, ss, ws = nsa_sample(hs, cache_nsa_cmp[j], cache_nsa_slc[j], state_nsa_win[j], page_table, *nsa_w)
            cmp_p.append(cp)
            slc_p.append(sp)
            win_p.append(wp)
            cmp_s.append(cs)
            slc_s.append(ss)
            win_s.append(ws)
        xp = layer_norm(DEEPNORM_ALPHA * xp + (1.0 + mp_[2]) * yp, ln_g[l, 0], ln_b[l, 0])
        xs = layer_norm(DEEPNORM_ALPHA * xs + (1.0 + ms_[2]) * ys, ln_g[l, 0], ln_b[l, 0])
        moe_w = (moe_w_router[l], moe_b_router[l], moe_w_gate[l], moe_w_up[l], moe_w_down[l],
                 moe_ws_gate[l], moe_ws_up[l], moe_ws_down[l])
        fp = moe_ffn(xp * (1.0 + mp_[4]) + mp_[3], *moe_w)
        fs = moe_ffn(xs * (1.0 + ms_[4]) + ms_[3], *moe_w)
        xp = layer_norm(DEEPNORM_ALPHA * xp + (1.0 + mp_[5]) * fp, ln_g[l, 1], ln_b[l, 1])
        xs = layer_norm(DEEPNORM_ALPHA * xs + (1.0 + ms_[5]) * fs, ln_g[l, 1], ln_b[l, 1])
    return (xp, xs, jnp.stack(mla_p), jnp.stack(cmp_p), jnp.stack(slc_p), jnp.stack(win_p),
            jnp.stack(mla_s), jnp.stack(cmp_s), jnp.stack(slc_s), jnp.stack(win_s))
```

```python
import functools
import math

import numpy as np
import jax
import jax.numpy as jnp
from jax import lax
from jax.experimental import pallas as pl
from jax.experimental.pallas import tpu as pltpu

F32 = jnp.float32
BF16 = jnp.bfloat16

D_MODEL = 1024
PAGE_SIZE = 128
MLA_HEADS = 16
MLA_Q_LORA = 512
MLA_KV_LORA = 256
MLA_NOPE = 64
MLA_ROPE = 32
MLA_VDIM = 64
MLA_C = MLA_KV_LORA + MLA_ROPE
ROPE_THETA = 10000.0
NSA_HEADS = 16
NSA_KV_HEADS = 2
NSA_GROUP = NSA_HEADS // NSA_KV_HEADS
NSA_HDIM = 64
N_BRANCH = 3
CMP_LEN = 32
CMP_STRIDE = 16
SEL_BLOCK = 64
SEL_TOPK = 16
WINDOW = 512
SEL_OVERLAP_W = (1.0, 2.0, 2.0, 2.0, 1.0)
REL_BUCKETS = 32
REL_MAX_DIST = 128
N_EXPERTS = 256
TOP_K = 8
N_EXPERT_GROUPS = 8
TOPK_GROUPS = 4
D_EXPERT = 256
ROUTED_SCALE = 2.5
DEPTH = 2
DEEPNORM_ALPHA = (2 * DEPTH) ** 0.25
LN_EPS = 1e-5
RMS_EPS = 1e-6
NEG_INF = -1e30
SEL_BIG = 1e9

V7X_VMEM_LIMIT_BYTES = 56 * 1024 * 1024
LANE = 128
MOE_ROWS = 128
DEC_PAGES = 16
DEC_QPAD = 8


def _cp(*sem):
    return pltpu.CompilerParams(dimension_semantics=sem, vmem_limit_bytes=V7X_VMEM_LIMIT_BYTES)


def _tile(m, pref):
    if m <= pref:
        return m
    for t in range(pref - pref % 8, 7, -8):
        if m % t == 0:
            return t
    return m


def _bdot(a, b):
    return jnp.dot(a, b, preferred_element_type=F32)


def _bdot_nt(a, b):
    return lax.dot_general(a, b, (((1,), (1,)), ((), ())), preferred_element_type=F32)


def _split(x):
    hi = x.astype(BF16)
    lo = (x - hi.astype(F32)).astype(BF16)
    return hi, lo


def _dot(x, w, hi=False):
    if not hi:
        return _bdot(x.astype(BF16), w.astype(BF16))
    xh, xl = _split(x)
    wh, wl = _split(w)
    return _bdot(xh, wh) + (_bdot(xl, wh) + _bdot(xh, wl))


def _sigmoid(x):
    return 1.0 / (1.0 + jnp.exp(-x))


def _mm_kernel(*refs, has_mod, has_bias, has_post, act_in, hi, post_rep):
    it = iter(refs)
    x_ref, w_ref = next(it), next(it)
    x = x_ref[...]
    if act_in == "silu":
        x = x.astype(F32)
        x = x * _sigmoid(x)
    if has_mod:
        sc_ref, sh_ref = next(it), next(it)
        x = x.astype(F32) * (1.0 + sc_ref[...]) + sh_ref[...]
    acc = _dot(x, w_ref[...], hi)
    if has_bias:
        acc = acc + next(it)[...]
    if has_post:
        p = next(it)[...]
        if post_rep > 1:
            p = jnp.concatenate([p] * post_rep, axis=1)
        acc = acc * p
    o_ref = next(it)
    o_ref[...] = acc.astype(o_ref.dtype)


def mm(x, w, *, mod=None, bias=None, post=None, act_in=None, hi=False, out_dtype=F32, tm=512, tn=None):
    m, k = x.shape
    n = w.shape[1]
    tm = _tile(m, tm)
    tn = n if tn is None else tn
    assert m % tm == 0 and n % tn == 0
    ins = [x, w]
    specs = [pl.BlockSpec((tm, k), lambda i, j: (i, 0)), pl.BlockSpec((k, tn), lambda i, j: (0, j))]

    def row_spec(a):
        if a.shape[0] == 1:
            return pl.BlockSpec((1, a.shape[1]), lambda i, j: (0, 0))
        assert a.shape[0] == m
        return pl.BlockSpec((tm, a.shape[1]), lambda i, j: (i, 0))

    if mod is not None:
        for a in mod:
            ins.append(a)
            specs.append(row_spec(a))
    if bias is not None:
        ins.append(bias)
        specs.append(pl.BlockSpec((1, tn), lambda i, j: (0, j)))
    post_rep = 1
    if post is not None:
        assert tn % post.shape[1] == 0
        post_rep = tn // post.shape[1]
        ins.append(post)
        specs.append(row_spec(post))
    kern = functools.partial(_mm_kernel, has_mod=mod is not None, has_bias=bias is not None,
                             has_post=post is not None, act_in=act_in, hi=hi, post_rep=post_rep)
    return pl.pallas_call(
        kern, grid=(m // tm, n // tn), in_specs=specs,
        out_specs=pl.BlockSpec((tm, tn), lambda i, j: (i, j)),
        out_shape=jax.ShapeDtypeStruct((m, n), out_dtype),
        compiler_params=_cp("parallel", "arbitrary"), name="mm")(*ins)


def _ln_res_kernel(*refs, n_y, has_next):
    it = iter(refs)
    x_ref = next(it)
    y = next(it)[...].astype(F32)
    for _ in range(n_y - 1):
        y = y + next(it)[...].astype(F32)
    gate_ref, g_ref, b_ref = next(it), next(it), next(it)
    if has_next:
        sc_ref, sh_ref = next(it), next(it)
    v = DEEPNORM_ALPHA * x_ref[...] + (1.0 + gate_ref[...]) * y
    mu = jnp.mean(v, axis=-1, keepdims=True)
    c = v - mu
    var = jnp.mean(c * c, axis=-1, keepdims=True)
    xn = c * lax.rsqrt(var + LN_EPS) * g_ref[...] + b_ref[...]
    o_ref = next(it)
    o_ref[...] = xn
    if has_next:
        h_ref = next(it)
        h_ref[...] = xn * (1.0 + sc_ref[...]) + sh_ref[...]


def ln_res(x, ys, gate, g, b, nxt=None, tm=512):
    m, d = x.shape
    tm = _tile(m, tm)
    assert m % tm == 0

    def row_spec(a):
        if a.shape[0] == 1:
            return pl.BlockSpec((1, d), lambda i: (0, 0))
        return pl.BlockSpec((tm, d), lambda i: (i, 0))

    tile = pl.BlockSpec((tm, d), lambda i: (i, 0))
    ins = [x] + list(ys) + [gate, g, b]
    specs = [tile] + [tile] * len(ys) + [row_spec(gate), row_spec(g), row_spec(b)]
    outs = [jax.ShapeDtypeStruct((m, d), F32)]
    ospecs = [tile]
    if nxt is not None:
        ins += list(nxt)
        specs += [row_spec(nxt[0]), row_spec(nxt[1])]
        outs.append(jax.ShapeDtypeStruct((m, d), F32))
        ospecs.append(tile)
    res = pl.pallas_call(
        functools.partial(_ln_res_kernel, n_y=len(ys), has_next=nxt is not None),
        grid=(m // tm,), in_specs=specs, out_specs=ospecs, out_shape=outs,
        compiler_params=_cp("parallel"), name="ln_res")(*ins)
    return res if nxt is not None else res[0]


def _mla_in_kernel(x_ref, sc_ref, sh_ref, w_ref, wr_ref, gq_ref, gkv_ref, cs_ref, cq_ref, rows_ref):
    h = (x_ref[...] * (1.0 + sc_ref[...]) + sh_ref[...]).astype(BF16)
    z = _bdot(h, w_ref[...].astype(BF16))
    zr = _bdot(h, wr_ref[...].astype(BF16))
    zq = z[:, :MLA_Q_LORA]
    zkv = z[:, MLA_Q_LORA:]
    cq = zq * lax.rsqrt(jnp.mean(zq * zq, axis=-1, keepdims=True) + RMS_EPS) * gq_ref[...]
    ckv = zkv * lax.rsqrt(jnp.mean(zkv * zkv, axis=-1, keepdims=True) + RMS_EPS) * gkv_ref[...]
    t = zr * cs_ref[...]
    kr = t + pltpu.roll(t, LANE - MLA_ROPE, axis=1)
    cq_ref[...] = cq.astype(cq_ref.dtype)
    rows_ref[:, :MLA_KV_LORA] = ckv
    rows_ref[:, MLA_KV_LORA:] = kr[:, :MLA_ROPE]


def mla_in(x, mod, w_main, w_rope, g_q, g_kv, cs, tm=512):
    m, d = x.shape
    tm = _tile(m, tm)

    def row_spec(a):
        if a.shape[0] == 1:
            return pl.BlockSpec((1, a.shape[1]), lambda i: (0, 0))
        return pl.BlockSpec((tm, a.shape[1]), lambda i: (i, 0))

    full = lambda a: pl.BlockSpec(a.shape, lambda i: (0, 0))
    return pl.pallas_call(
        _mla_in_kernel, grid=(m // tm,),
        in_specs=[row_spec(x), row_spec(mod[0]), row_spec(mod[1]), full(w_main), full(w_rope),
                  full(g_q), full(g_kv), row_spec(cs)],
        out_specs=[pl.BlockSpec((tm, MLA_Q_LORA), lambda i: (i, 0)), pl.BlockSpec((tm, MLA_C), lambda i: (i, 0))],
        out_shape=[jax.ShapeDtypeStruct((m, MLA_Q_LORA), BF16), jax.ShapeDtypeStruct((m, MLA_C), F32)],
        compiler_params=_cp("parallel"), name="mla_in")(x, mod[0], mod[1], w_main, w_rope, g_q, g_kv, cs)


def _flash_mla_kernel(q_ref, k_ref, v_ref, o_ref, m_sc, l_sc, acc_sc, *, t):
    qi = pl.program_id(1)
    ki = pl.program_id(2)

    @pl.when(ki == 0)
    def _():
        m_sc[...] = jnp.full(m_sc.shape, NEG_INF, F32)
        l_sc[...] = jnp.zeros(l_sc.shape, F32)
        acc_sc[...] = jnp.zeros(acc_sc.shape, F32)

    def step(diag):
        v = v_ref[...]
        for h in range(2):
            s = _bdot_nt(q_ref[:, h * LANE:(h + 1) * LANE], k_ref[:, h * LANE:(h + 1) * LANE])
            if diag:
                row = lax.broadcasted_iota(jnp.int32, (t, t), 0)
                col = lax.broadcasted_iota(jnp.int32, (t, t), 1)
                s = jnp.where(col <= row, s, NEG_INF)
            m_prev = m_sc[h]
            m_new = jnp.maximum(m_prev, jnp.max(s, axis=-1, keepdims=True))
            alpha = jnp.exp(m_prev - m_new)
            p = jnp.exp(s - m_new)
            l_sc[h] = alpha * l_sc[h] + jnp.sum(p, axis=-1, keepdims=True)
            acc_sc[h] = alpha * acc_sc[h] + _bdot(p.astype(BF16), v)
            m_sc[h] = m_new

    @pl.when(ki < qi)
    def _():
        step(False)

    @pl.when(ki == qi)
    def _():
        step(True)
        lane = lax.broadcasted_iota(jnp.int32, (t, LANE), 1)
        o = jnp.where(lane < MLA_VDIM, acc_sc[0] / l_sc[0], acc_sc[1] / l_sc[1])
        o_ref[...] = o.astype(o_ref.dtype)


def flash_mla(q, kv, s, t=512):
    t = min(t, s)
    nq = s // t
    hp = MLA_HEADS // 2
    v_col0 = MLA_HEADS
    return pl.pallas_call(
        functools.partial(_flash_mla_kernel, t=t), grid=(hp, nq, nq),
        in_specs=[pl.BlockSpec((t, 2 * LANE), lambda h, i, j: (i, h)),
                  pl.BlockSpec((t, 2 * LANE), lambda h, i, j: (jnp.minimum(i, j), h)),
                  pl.BlockSpec((t, LANE), lambda h, i, j: (jnp.minimum(i, j), v_col0 + h))],
        out_specs=pl.BlockSpec((t, LANE), lambda h, i, j: (i, h)),
        out_shape=jax.ShapeDtypeStruct((s, MLA_HEADS * MLA_VDIM), BF16),
        scratch_shapes=[pltpu.VMEM((2, t, 1), F32), pltpu.VMEM((2, t, 1), F32), pltpu.VMEM((2, t, LANE), F32)],
        compiler_params=_cp("parallel", "parallel", "arbitrary"), name="flash_mla")(q, kv, kv)


def _head_mm_kernel(x_ref, w_ref, o_ref):
    o_ref[0] = _dot(x_ref[...], w_ref[0]).astype(o_ref.dtype)


def head_mm(x, w, out_dtype):
    m = x.shape[0]
    h, k, n = w.shape
    return pl.pallas_call(
        _head_mm_kernel, grid=(h,),
        in_specs=[pl.BlockSpec((m, k), lambda i: (0, i)), pl.BlockSpec((1, k, n), lambda i: (i, 0, 0))],
        out_specs=pl.BlockSpec((1, m, n), lambda i: (i, 0, 0)),
        out_shape=jax.ShapeDtypeStruct((h, m, n), out_dtype),
        compiler_params=_cp("parallel"), name="head_mm")(x, w)


def _head_out_kernel(o_ref, wv_ref, wo_ref, y_ref):
    @pl.when(pl.program_id(0) == 0)
    def _():
        y_ref[...] = jnp.zeros(y_ref.shape, F32)

    o = _dot(o_ref[0], wv_ref[0])
    y_ref[...] += _dot(o, wo_ref[0])


def head_out(o_lat, w_uv, w_o):
    h, m, c = o_lat.shape
    v, d = w_o.shape[1:]
    return pl.pallas_call(
        _head_out_kernel, grid=(h,),
        in_specs=[pl.BlockSpec((1, m, c), lambda i: (i, 0, 0)), pl.BlockSpec((1, c, v), lambda i: (i, 0, 0)),
                  pl.BlockSpec((1, v, d), lambda i: (i, 0, 0))],
        out_specs=pl.BlockSpec((m, d), lambda i: (0, 0)),
        out_shape=jax.ShapeDtypeStruct((m, d), F32),
        compiler_params=_cp("arbitrary"), name="head_out")(o_lat, w_uv, w_o)


def _mla_dec_kernel(pt_ref, q_ref, new_ref, *rest, n_pg, ds):
    pages = rest[:n_pg]
    o_ref, m_sc, l_sc, acc_sc = rest[n_pg:]
    j = pl.program_id(1)
    nr = MLA_HEADS * DEC_QPAD
    q = q_ref[...].reshape(nr, MLA_C)

    @pl.when(j == 0)
    def _():
        m_sc[...] = jnp.full(m_sc.shape, NEG_INF, F32)
        l_sc[...] = jnp.zeros(l_sc.shape, F32)
        acc_sc[...] = jnp.zeros(acc_sc.shape, F32)

    kb = [p[0].astype(BF16) for p in pages]
    s = jnp.concatenate([_bdot_nt(q, k) for k in kb], axis=1)
    m_prev = m_sc[...]
    m_new = jnp.maximum(m_prev, jnp.max(s, axis=-1, keepdims=True))
    alpha = jnp.exp(m_prev - m_new)
    p = jnp.exp(s - m_new)
    l_sc[...] = alpha * l_sc[...] + jnp.sum(p, axis=-1, keepdims=True)
    pv = _bdot(p[:, :PAGE_SIZE].astype(BF16), kb[0][:, :MLA_KV_LORA])
    for i in range(1, n_pg):
        pv = pv + _bdot(p[:, i * PAGE_SIZE:(i + 1) * PAGE_SIZE].astype(BF16), kb[i][:, :MLA_KV_LORA])
    acc_sc[...] = alpha * acc_sc[...] + pv
    m_sc[...] = m_new

    @pl.when(j == pl.num_programs(1) - 1)
    def _():
        kn = new_ref[0].astype(BF16)
        sn = _bdot_nt(q, kn)
        qidx = lax.broadcasted_iota(jnp.int32, sn.shape, 0) % DEC_QPAD
        kidx = lax.broadcasted_iota(jnp.int32, sn.shape, 1)
        ok = (kidx <= qidx) & (kidx < ds)
        sn = jnp.where(ok, sn, NEG_INF)
        m_prev = m_sc[...]
        m_new = jnp.maximum(m_prev, jnp.max(sn, axis=-1, keepdims=True))
        alpha = jnp.exp(m_prev - m_new)
        pn = jnp.where(ok, jnp.exp(sn - m_new), 0.0)
        l = alpha * l_sc[...] + jnp.sum(pn, axis=-1, keepdims=True)
        acc = alpha * acc_sc[...] + _bdot(pn.astype(BF16), kn[:, :MLA_KV_LORA])
        o_ref[...] = (acc / l).reshape(o_ref.shape).astype(o_ref.dtype)


def mla_dec(q_lat, new_rows, cache, page_table, ds):
    h, b, qp, c = q_lat.shape
    n_pages = page_table.shape[1]
    n_pg = min(DEC_PAGES, n_pages)
    assert n_pages % n_pg == 0
    page_specs = [pl.BlockSpec((1, PAGE_SIZE, c), functools.partial(
        lambda bi, j, pt, i: (pt[bi, j * n_pg + i], 0, 0), i=i)) for i in range(n_pg)]
    gs = pltpu.PrefetchScalarGridSpec(
        num_scalar_prefetch=1, grid=(b, n_pages // n_pg),
        in_specs=[pl.BlockSpec((h, 1, qp, c), lambda bi, j, pt: (0, bi, 0, 0)),
                  pl.BlockSpec((1, qp, c), lambda bi, j, pt: (bi, 0, 0))] + page_specs,
        out_specs=pl.BlockSpec((h, 1, qp, MLA_KV_LORA), lambda bi, j, pt: (0, bi, 0, 0)),
        scratch_shapes=[pltpu.VMEM((h * qp, 1), F32), pltpu.VMEM((h * qp, 1), F32),
                        pltpu.VMEM((h * qp, MLA_KV_LORA), F32)])
    return pl.pallas_call(
        functools.partial(_mla_dec_kernel, n_pg=n_pg, ds=ds), grid_spec=gs,
        out_shape=jax.ShapeDtypeStruct((h, b, qp, MLA_KV_LORA), F32),
        compiler_params=_cp("parallel", "arbitrary"), name="mla_dec")(
            page_table, q_lat, new_rows, *([cache] * n_pg))


def _router_kernel(h_ref, w_ref, b_ref, e_ref, p_ref):
    tm = h_ref.shape[0]
    per = N_EXPERTS // N_EXPERT_GROUPS
    scores = _sigmoid(_dot(h_ref[...], w_ref[...], hi=True))
    biased = scores + b_ref[...]
    lane = lax.broadcasted_iota(jnp.int32, (tm, N_EXPERTS), 1)
    lane_f = lane.astype(F32)
    grp_id = lane // per
    ninf = -jnp.inf

    grp = []
    for g in range(N_EXPERT_GROUPS):
        vg = jnp.where(grp_id == g, biased, ninf)
        m1 = jnp.max(vg, axis=-1, keepdims=True)
        i1 = jnp.min(jnp.where(vg == m1, lane_f, float(N_EXPERTS)), axis=-1, keepdims=True)
        m2 = jnp.max(jnp.where(lane_f == i1, ninf, vg), axis=-1, keepdims=True)
        grp.append(m1 + m2)
    gmask = jnp.zeros((tm, N_EXPERTS), jnp.bool_)
    for g in range(N_EXPERT_GROUPS):
        rank = jnp.zeros((tm, 1), F32)
        for g2 in range(N_EXPERT_GROUPS):
            if g2 == g:
                continue
            ahead = (grp[g2] > grp[g]) | ((grp[g2] == grp[g]) & (g2 < g))
            rank = rank + jnp.where(ahead, 1.0, 0.0)
        gmask = gmask | ((grp_id == g) & (rank < float(TOPK_GROUPS)))
    v = jnp.where(gmask, biased, NEG_INF)

    out_lane = lax.broadcasted_iota(jnp.int32, (tm, LANE), 1)
    e_out = jnp.zeros((tm, LANE), F32)
    w_out = jnp.zeros((tm, LANE), F32)
    for k in range(TOP_K):
        mk = jnp.max(v, axis=-1, keepdims=True)
        ik = jnp.min(jnp.where(v == mk, lane_f, float(N_EXPERTS)), axis=-1, keepdims=True)
        hit = lane_f == ik
        wk = jnp.sum(jnp.where(hit, scores, 0.0), axis=-1, keepdims=True)
        v = jnp.where(hit, ninf, v)
        e_out = jnp.where(out_lane == k, ik, e_out)
        w_out = jnp.where(out_lane == k, wk, w_out)
    w_out = w_out / jnp.sum(w_out, axis=-1, keepdims=True) * ROUTED_SCALE
    e_ref[...] = e_out.astype(jnp.int32)
    p_ref[...] = w_out


def router(h, w_router, b_router, tm=256):
    t, d = h.shape
    tm = _tile(t, tm)
    assert t % tm == 0
    return pl.pallas_call(
        _router_kernel, grid=(t // tm,),
        in_specs=[pl.BlockSpec((tm, d), lambda i: (i, 0)), pl.BlockSpec((d, N_EXPERTS), lambda i: (0, 0)),
                  pl.BlockSpec((1, N_EXPERTS), lambda i: (0, 0))],
        out_specs=[pl.BlockSpec((tm, LANE), lambda i: (i, 0)), pl.BlockSpec((tm, LANE), lambda i: (i, 0))],
        out_shape=[jax.ShapeDtypeStruct((t, LANE), jnp.int32), jax.ShapeDtypeStruct((t, LANE), F32)],
        compiler_params=_cp("parallel"), name="router")(h, w_router, b_router)


def _experts_kernel(be_ref, nu_ref, x_ref, rw_ref, wg_ref, wu_ref, wd_ref, y_ref, wgu_sc, wd_sc):
    i = pl.program_id(0)
    e = be_ref[i]
    prev = be_ref[jnp.maximum(i - 1, 0)]

    @pl.when((i == 0) | (e != prev))
    def _():
        wgu_sc[:, :D_EXPERT] = wg_ref[0].astype(BF16)
        wgu_sc[:, D_EXPERT:] = wu_ref[0].astype(BF16)
        wd_sc[...] = wd_ref[0].astype(BF16)

    @pl.when(i < nu_ref[0])
    def _():
        gu = _bdot(x_ref[...], wgu_sc[...])
        g = gu[:, :D_EXPERT]
        a = (g * _sigmoid(g) * gu[:, D_EXPERT:]).astype(BF16)
        y_ref[...] = _bdot(a, wd_sc[...]) * rw_ref[...]

    @pl.when(i >= nu_ref[0])
    def _():
        y_ref[...] = jnp.zeros(y_ref.shape, F32)


def experts(xb, row_w, blk_e, n_used, w_gate, w_up, w_down):
    n_rows, d = xb.shape
    n_blk = n_rows // MOE_ROWS
    gs = pltpu.PrefetchScalarGridSpec(
        num_scalar_prefetch=2, grid=(n_blk,),
        in_specs=[pl.BlockSpec((MOE_ROWS, d), lambda i, be, nu: (i, 0)),
                  pl.BlockSpec((MOE_ROWS, 1), lambda i, be, nu: (i, 0)),
                  pl.BlockSpec((1, d, D_EXPERT), lambda i, be, nu: (be[i], 0, 0)),
                  pl.BlockSpec((1, d, D_EXPERT), lambda i, be, nu: (be[i], 0, 0)),
                  pl.BlockSpec((1, D_EXPERT, d), lambda i, be, nu: (be[i], 0, 0))],
        out_specs=pl.BlockSpec((MOE_ROWS, d), lambda i, be, nu: (i, 0)),
        scratch_shapes=[pltpu.VMEM((d, 2 * D_EXPERT), BF16), pltpu.VMEM((D_EXPERT, d), BF16)])
    return pl.pallas_call(
        _experts_kernel, grid_spec=gs, out_shape=jax.ShapeDtypeStruct((n_rows, d), F32),
        compiler_params=_cp("arbitrary"), name="experts")(blk_e, n_used, xb, row_w, w_gate, w_up, w_down)


def _shared_kernel(x_ref, wg_ref, wu_ref, wd_ref, y_ref):
    x = x_ref[...].astype(BF16)
    g = _bdot(x, wg_ref[...].astype(BF16))
    u = _bdot(x, wu_ref[...].astype(BF16))
    y_ref[...] = _dot(g * _sigmoid(g) * u, wd_ref[...])


def shared_ffn(x, wg, wu, wd, tm=512):
    t, d = x.shape
    tm = _tile(t, tm)
    full = lambda a: pl.BlockSpec(a.shape, lambda i: (0, 0))
    return pl.pallas_call(
        _shared_kernel, grid=(t // tm,),
        in_specs=[pl.BlockSpec((tm, d), lambda i: (i, 0)), full(wg), full(wu), full(wd)],
        out_specs=pl.BlockSpec((tm, d), lambda i: (i, 0)),
        out_shape=jax.ShapeDtypeStruct((t, d), F32),
        compiler_params=_cp("parallel"), name="shared_ffn")(x, wg, wu, wd)


def moe_ffn(h, w_router, b_router, w_gate, w_up, w_down, ws_gate, ws_up, ws_down):
    t, d = h.shape
    top_e, top_w = router(h, w_router, b_router[None])
    top_e, top_w = top_e[:, :TOP_K], top_w[:, :TOP_K]
    n_as = t * TOP_K
    e_flat = top_e.reshape(-1)
    order = jnp.argsort(e_flat, stable=True)
    e_s = e_flat[order]
    counts = jnp.zeros((N_EXPERTS,), jnp.int32).at[e_flat].add(1)
    starts = jnp.cumsum(counts) - counts
    padded = (counts + MOE_ROWS - 1) // MOE_ROWS * MOE_ROWS
    pend = jnp.cumsum(padded)
    dest = (pend - padded)[e_s] + jnp.arange(n_as, dtype=jnp.int32) - starts[e_s]
    n_blk = -(-n_as // MOE_ROWS) + N_EXPERTS
    n_rows = n_blk * MOE_ROWS
    row_tok = jnp.full((n_rows,), t, jnp.int32).at[dest].set((order // TOP_K).astype(jnp.int32))
    row_w = jnp.zeros((n_rows,), F32).at[dest].set(top_w.reshape(-1)[order])
    blk_e = jnp.minimum(jnp.searchsorted(pend, jnp.arange(n_blk, dtype=jnp.int32) * MOE_ROWS, side="right"),
                        N_EXPERTS - 1).astype(jnp.int32)
    n_used = (pend[-1:] // MOE_ROWS).astype(jnp.int32)
    pos = jnp.zeros((n_as,), jnp.int32).at[order].set(dest.astype(jnp.int32)).reshape(t, TOP_K)
    hb = jnp.concatenate([h.astype(BF16), jnp.zeros((1, d), BF16)])
    xb = hb[row_tok]
    yb = experts(xb, row_w[:, None], blk_e, n_used, w_gate, w_up, w_down)
    routed = yb[pos].sum(axis=1)
    return routed, shared_ffn(h, ws_gate, ws_up, ws_down)


_ROPE_SWAP = np.concatenate([np.arange(MLA_ROPE // 2, MLA_ROPE), np.arange(MLA_ROPE // 2)])


def _rope_cos_sin(pos):
    half = MLA_ROPE // 2
    inv = ROPE_THETA ** (-jnp.arange(half, dtype=F32) / half)
    ang = pos.astype(F32)[:, None] * inv
    return jnp.cos(ang), jnp.sin(ang)


def _rope_tables(pos):
    cos, sin = _rope_cos_sin(pos)
    n = pos.shape[0]
    rot = jnp.concatenate([cos, cos, -sin, sin], axis=1)
    cs_k = jnp.concatenate([rot, jnp.zeros((n, LANE - 2 * MLA_ROPE), F32)], axis=1)
    scale = (MLA_NOPE + MLA_ROPE) ** -0.5
    t_q = scale * jnp.concatenate([jnp.ones((n, MLA_NOPE), F32), rot], axis=1)
    return cs_k, t_q


def _mla_weights(w_in, w_uq, w_uk, w_uv):
    d = w_in.shape[0]
    nq, nkv = MLA_Q_LORA, MLA_KV_LORA
    w_main = w_in[:, :nq + nkv]
    wr = w_in[:, nq + nkv:]
    w_rope = jnp.concatenate([wr, wr[:, _ROPE_SWAP], jnp.zeros((d, LANE - 2 * MLA_ROPE), F32)], axis=1)
    wq = jnp.concatenate([w_uq, w_uq[:, :, MLA_NOPE + _ROPE_SWAP]], axis=-1).reshape(nq, MLA_HEADS * LANE)
    eye = jnp.eye(MLA_ROPE, dtype=F32)
    k_top = jnp.pad(w_uk, ((0, 0), (0, 0), (0, LANE - MLA_NOPE)))
    k_bot = jnp.broadcast_to(jnp.concatenate([jnp.zeros((MLA_ROPE, MLA_NOPE), F32), eye, eye], axis=1)[:, None, :],
                             (MLA_ROPE, MLA_HEADS, LANE))
    wk = jnp.concatenate([k_top, k_bot], axis=0).reshape(MLA_C, MLA_HEADS * LANE)
    wv = jnp.pad(w_uv.reshape(nkv, MLA_HEADS * MLA_VDIM), ((0, MLA_ROPE), (0, 0)))
    w_kv = jnp.concatenate([wk, wv], axis=1)
    a_top = jnp.pad(jnp.transpose(w_uk, (1, 2, 0)), ((0, 0), (0, 0), (0, MLA_ROPE)))
    a_bot = jnp.broadcast_to(jnp.concatenate([jnp.zeros((MLA_ROPE, nkv), F32), eye], axis=1)[None],
                             (MLA_HEADS, MLA_ROPE, MLA_C))
    w_abs = jnp.concatenate([a_top, a_bot, a_bot], axis=1)
    return w_main, w_rope, wq, w_kv, w_abs, jnp.transpose(w_uv, (1, 0, 2))


def mla_layer(xp, xs, mod_p, mod_s, cache, page_table, ds, w_in, g_q, g_kv, w_uq, w_uk, w_uv, w_o):
    s = xp.shape[0]
    b = page_table.shape[0]
    past = page_table.shape[1] * PAGE_SIZE
    w_main, w_rope, wq, w_kv, w_abs, w_uv_h = _mla_weights(w_in, w_uq, w_uk, w_uv)
    csk_p, tq_p = _rope_tables(jnp.arange(s))
    csk_s, tq_s = _rope_tables(past + jnp.arange(b * ds) % ds)
    w_o2 = w_o.reshape(MLA_HEADS * MLA_VDIM, -1)
    cq_p, rows_p = mla_in(xp, mod_p, w_main, w_rope, g_q[None], g_kv[None], csk_p)
    q_p = mm(cq_p, wq, post=tq_p, out_dtype=BF16)
    kv_p = mm(rows_p, w_kv, out_dtype=BF16)
    o_p = flash_mla(q_p, kv_p, s)
    y_p = mm(o_p, w_o2)
    cq_s, rows_s = mla_in(xs, mod_s, w_main, w_rope, g_q[None], g_kv[None], csk_s)
    q_s = mm(cq_s, wq, post=tq_s, out_dtype=BF16)
    q_lat = head_mm(q_s, w_abs, BF16).reshape(MLA_HEADS, b, ds, MLA_C)
    q_lat = jnp.pad(q_lat, ((0, 0), (0, 0), (0, DEC_QPAD - ds), (0, 0)))
    new_rows = jnp.pad(rows_s.reshape(b, ds, MLA_C), ((0, 0), (0, DEC_QPAD - ds), (0, 0)))
    o_lat = mla_dec(q_lat, new_rows, cache, page_table, ds)
    y_s = head_out(o_lat.reshape(MLA_HEADS, b * DEC_QPAD, MLA_KV_LORA), w_uv_h, w_o)
    y_s = y_s.reshape(b, DEC_QPAD, -1)[:, :ds].reshape(b * ds, -1)
    return y_p, y_s, rows_p, rows_s


NSA_NQ = NSA_HEADS * NSA_HDIM
NSA_ROW = 2 * NSA_KV_HEADS * NSA_HDIM
NSA_FLASH_T = 256
NSA_CMP_TQ = 128
CMP_WIN = 32


def _nsa_in_kernel(x_ref, sc_ref, sh_ref, w_ref, wg_ref, q_ref, c_ref, s_ref, w_out_ref, g_ref):
    h = (x_ref[...] * (1.0 + sc_ref[...]) + sh_ref[...]).astype(BF16)
    z = _bdot(h, w_ref[...].astype(BF16))
    q_ref[...] = (z[:, :NSA_NQ] * NSA_HDIM ** -0.5).astype(BF16)
    c_ref[...] = z[:, NSA_NQ:NSA_NQ + NSA_ROW]
    s_ref[...] = z[:, NSA_NQ + NSA_ROW:NSA_NQ + 2 * NSA_ROW]
    w_out_ref[...] = z[:, NSA_NQ + 2 * NSA_ROW:]
    g_ref[...] = _sigmoid(_bdot(h, wg_ref[...].astype(BF16)))


def nsa_in(x, mod, w_main, w_gate, tm=512):
    m, d = x.shape
    tm = _tile(m, tm)

    def row_spec(a):
        if a.shape[0] == 1:
            return pl.BlockSpec((1, a.shape[1]), lambda i: (0, 0))
        return pl.BlockSpec((tm, a.shape[1]), lambda i: (i, 0))

    full = lambda a: pl.BlockSpec(a.shape, lambda i: (0, 0))
    widths = (NSA_NQ, NSA_ROW, NSA_ROW, NSA_ROW, LANE)
    dts = (BF16, F32, F32, F32, F32)
    return pl.pallas_call(
        _nsa_in_kernel, grid=(m // tm,),
        in_specs=[row_spec(x), row_spec(mod[0]), row_spec(mod[1]), full(w_main), full(w_gate)],
        out_specs=[pl.BlockSpec((tm, w), lambda i: (i, 0)) for w in widths],
        out_shape=[jax.ShapeDtypeStruct((m, w), dt) for w, dt in zip(widths, dts)],
        compiler_params=_cp("parallel"), name="nsa_in")(x, mod[0], mod[1], w_main, w_gate)


def _cmp_paged_kernel(pt_ref, w_ref, b_ref, *rest, n_pg):
    pages = rest[:n_pg]
    o_ref = rest[n_pg]
    x = jnp.concatenate([p[0] for p in pages], axis=0).astype(BF16)
    o_ref[0] = _bdot(x, w_ref[...]) + b_ref[...]


def cmp_paged(cache, page_table, w_big, bias):
    b, n_pages = page_table.shape
    n_pg = min(DEC_PAGES, n_pages)
    cpp = PAGE_SIZE // CMP_STRIDE
    kdim = cache.shape[-1]
    page_specs = [pl.BlockSpec((1, cpp, kdim), functools.partial(
        lambda bi, j, pt, i: (pt[bi, j * n_pg + i], 0, 0), i=i)) for i in range(n_pg)]
    gs = pltpu.PrefetchScalarGridSpec(
        num_scalar_prefetch=1, grid=(b, n_pages // n_pg),
        in_specs=[pl.BlockSpec(w_big.shape, lambda bi, j, pt: (0, 0)),
                  pl.BlockSpec(bias.shape, lambda bi, j, pt: (0, 0))] + page_specs,
        out_specs=pl.BlockSpec((1, n_pg * cpp, w_big.shape[1]), lambda bi, j, pt: (bi, j, 0)))
    return pl.pallas_call(
        functools.partial(_cmp_paged_kernel, n_pg=n_pg), grid_spec=gs,
        out_shape=jax.ShapeDtypeStruct((b, n_pages * cpp, w_big.shape[1]), F32),
        compiler_params=_cp("parallel", "arbitrary"), name="cmp_paged")(
            page_table, w_big, bias, *([cache] * n_pg))


def _group_halves(x, g, lane):
    keep = jnp.where((lane // NSA_HDIM) == g, x, jnp.zeros_like(x))
    other = pltpu.roll(keep.astype(F32), NSA_HDIM, axis=1).astype(x.dtype)
    if g == 0:
        return keep, other
    return other, keep


def _nsa_cmp_kernel(q_ref, ab_ref, nb_ref, m_ref, o_ref, imp_ref, *, tq, base, n_t):
    nc = ab_ref.shape[1]
    q0 = base + (pl.program_id(1) if n_t > 1 else 0) * tq
    ab = ab_ref[0]
    kv = ab[:, :NSA_ROW] + pltpu.roll(ab[:, NSA_ROW:], nc - 1, axis=0)
    lane = lax.broadcasted_iota(jnp.int32, (nc, LANE), 1)
    kk = kv[:, :LANE].astype(BF16)
    vv = kv[:, LANE:].astype(BF16)
    qpos = q0 + lax.broadcasted_iota(jnp.int32, (tq, nc), 0)
    c_end = lax.broadcasted_iota(jnp.int32, (tq, nc), 1) * CMP_STRIDE + (CMP_LEN - 1)
    ok = qpos >= c_end
    er = lax.broadcasted_iota(jnp.int32, (2 * CMP_WIN, nc), 0) % CMP_WIN
    en = lax.broadcasted_iota(jnp.int32, (2 * CMP_WIN, nc), 1)
    e2 = jnp.where(en - q0 // CMP_STRIDE + CMP_WIN // 2 == er, 1.0, 0.0).astype(BF16)
    mmat = m_ref[...]
    for g in range(NSA_KV_HEADS):
        k_lo, k_hi = _group_halves(kk, g, lane)
        v_lo, v_hi = _group_halves(vv, g, lane)
        psum = jnp.zeros((tq, nc), F32)
        for pr in range(NSA_GROUP // 2):
            col = (g * (NSA_GROUP // 2) + pr) * LANE
            q2 = q_ref[:, col:col + LANE]
            pn = []
            for hh, kh in enumerate((k_lo, k_hi)):
                head = 2 * (g * (NSA_GROUP // 2) + pr) + hh
                s = _bdot_nt(q2, kh) + _bdot(nb_ref[head], e2)
                s = jnp.where(ok, s, NEG_INF)
                mx = jnp.max(s, axis=-1, keepdims=True)
                p = jnp.where(ok, jnp.exp(s - mx), 0.0)
                l = jnp.sum(p, axis=-1, keepdims=True)
                p = p * jnp.where(l > 0.0, 1.0 / l, 0.0)
                psum = psum + p
                pn.append(p.astype(BF16))
            o_ref[:, col:col + LANE] = _bdot(pn[0], v_lo) + _bdot(pn[1], v_hi)
        p1 = psum.astype(BF16)
        p2, p3 = _split(psum - p1.astype(F32))
        imp_ref[g] = _bdot(p1, mmat) + (_bdot(p2, mmat) + _bdot(p3, mmat))


def nsa_cmp(q, ab, nb_tab, m_mat, tq, base):
    rows = q.shape[0]
    nb, nc = ab.shape[:2]
    n_t = rows // (nb * tq)
    nsb = m_mat.shape[1]
    return pl.pallas_call(
        functools.partial(_nsa_cmp_kernel, tq=tq, base=base, n_t=n_t), grid=(nb, n_t),
        in_specs=[pl.BlockSpec((tq, NSA_NQ), lambda b, t: (b * n_t + t, 0)),
                  pl.BlockSpec((1, nc, ab.shape[2]), lambda b, t: (b, 0, 0)),
                  pl.BlockSpec((NSA_HEADS, tq, 2 * CMP_WIN), lambda b, t: (0, 0, 0)),
                  pl.BlockSpec(m_mat.shape, lambda b, t: (0, 0))],
        out_specs=[pl.BlockSpec((tq, NSA_NQ), lambda b, t: (b * n_t + t, 0)),
                   pl.BlockSpec((NSA_KV_HEADS, tq, nsb), lambda b, t: (0, b * n_t + t, 0))],
        out_shape=[jax.ShapeDtypeStruct((rows, NSA_NQ), F32),
                   jax.ShapeDtypeStruct((NSA_KV_HEADS, rows, nsb), F32)],
        compiler_params=_cp("parallel", "arbitrary"), name="nsa_cmp")(q, ab, nb_tab, m_mat)


def _sel_topk_kernel(imp_ref, o_ref, *, tq, base, period, k_sel):
    nsb = imp_ref.shape[2]
    row = pl.program_id(1) * tq + lax.broadcasted_iota(jnp.int32, (tq, nsb), 0)
    qpos = base + row % period
    blk = lax.broadcasted_iota(jnp.int32, (tq, nsb), 1)
    blk_f = blk.astype(F32)
    cur = qpos // SEL_BLOCK
    forced = (blk == 0) | (blk == cur) | (blk == cur - 1)
    valid = blk * SEL_BLOCK <= qpos
    v = jnp.where(forced, SEL_BIG, jnp.where(valid, imp_ref[0], -SEL_BIG))
    sel = jnp.zeros((tq, nsb), F32)
    for _ in range(k_sel):
        mx = jnp.max(v, axis=-1, keepdims=True)
        ik = jnp.min(jnp.where(v == mx, blk_f, float(nsb)), axis=-1, keepdims=True)
        hit = blk_f == ik
        sel = jnp.where(hit, 1.0, sel)
        v = jnp.where(hit, -jnp.inf, v)
    o_ref[0] = sel.astype(o_ref.dtype)


def sel_topk(imp, tq, base, period, k_sel):
    g, rows, nsb = imp.shape
    tq = min(tq, rows)
    return pl.pallas_call(
        functools.partial(_sel_topk_kernel, tq=tq, base=base, period=period, k_sel=k_sel),
        grid=(g, rows // tq),
        in_specs=[pl.BlockSpec((1, tq, nsb), lambda gi, t: (gi, t, 0))],
        out_specs=pl.BlockSpec((1, tq, nsb), lambda gi, t: (gi, t, 0)),
        out_shape=jax.ShapeDtypeStruct((g, rows, nsb), BF16),
        compiler_params=_cp("parallel", "parallel"), name="sel_topk")(imp)


def _nsa_flash_kernel(*refs, mode, t, n_back):
    if mode == "sel":
        q_ref, kv_ref, tt_ref, msk_ref, o_ref, m_sc, l_sc, acc_sc = refs
    else:
        q_ref, kv_ref, tt_ref, o_ref, m_sc, l_sc, acc_sc = refs
    g = pl.program_id(0)
    qi = pl.program_id(1)
    j = pl.program_id(2)
    kt = j if mode == "sel" else qi - n_back + j
    off = qi - kt
    half = NSA_GROUP // 2

    @pl.when(j == 0)
    def _():
        m_sc[...] = jnp.full(m_sc.shape, NEG_INF, F32)
        l_sc[...] = jnp.zeros(l_sc.shape, F32)
        acc_sc[...] = jnp.zeros(acc_sc.shape, F32)

    def step(near, edge):
        lane = lax.broadcasted_iota(jnp.int32, (t, LANE), 1)
        kv = kv_ref[...]
        kk, vv = kv[:, :LANE], kv[:, LANE:]

        def halves(x):
            keep = jnp.where((lane // NSA_HDIM) == g, x, jnp.zeros_like(x))
            other = pltpu.roll(keep.astype(F32), NSA_HDIM, axis=1).astype(x.dtype)
            return jnp.where(g == 0, keep, other), jnp.where(g == 0, other, keep)

        k_lo, k_hi = halves(kk)
        v_lo, v_hi = halves(vv)
        row = lax.broadcasted_iota(jnp.int32, (t, t), 0)
        col = lax.broadcasted_iota(jnp.int32, (t, t), 1)
        negm = None
        ok = None
        if mode == "sel":
            nsb = msk_ref.shape[2]
            eb = lax.broadcasted_iota(jnp.int32, (nsb, t), 0)
            ej = lax.broadcasted_iota(jnp.int32, (nsb, t), 1)
            ex = jnp.where(eb == (kt * t + ej) // SEL_BLOCK, 1.0, 0.0).astype(BF16)
            negm = (_bdot(msk_ref[0], ex) - 1.0) * (-NEG_INF)
            if edge == "diag":
                negm = jnp.where(col <= row, negm, NEG_INF)
        else:
            if edge == "diag":
                ok = col <= row
            elif edge == "far":
                ok = col > row
        for pr in range(half):
            q2 = q_ref[:, pr * LANE:(pr + 1) * LANE]
            pb = []
            al = []
            for hh, kh in enumerate((k_lo, k_hi)):
                hd = 2 * pr + hh
                s = _bdot_nt(q2, kh)
                if near is not None:
                    s = s + tt_ref[hd, near]
                if negm is not None:
                    s = s + negm
                if ok is not None:
                    s = jnp.where(ok, s, NEG_INF)
                m_prev = m_sc[hd]
                m_new = jnp.maximum(m_prev, jnp.max(s, axis=-1, keepdims=True))
                alpha = jnp.exp(m_prev - m_new)
                p = jnp.exp(s - m_new)
                if ok is not None:
                    p = jnp.where(ok, p, 0.0)
                l_sc[hd] = alpha * l_sc[hd] + jnp.sum(p, axis=-1, keepdims=True)
                m_sc[hd] = m_new
                pb.append(p.astype(BF16))
                al.append(alpha)
            alpha2 = jnp.where(lane < NSA_HDIM, al[0], al[1])
            acc_sc[pr] = alpha2 * acc_sc[pr] + (_bdot(pb[0], v_lo) + _bdot(pb[1], v_hi))

    if mode == "sel":
        @pl.when(off >= 2)
        def _():
            step(None, None)
    else:
        @pl.when((off == 2) & (kt >= 0))
        def _():
            step(None, "far")

    @pl.when((off == 1) & (kt >= 0))
    def _():
        step(1, None)

    @pl.when(off == 0)
    def _():
        step(0, "diag")
        lane = lax.broadcasted_iota(jnp.int32, (t, LANE), 1)
        for pr in range(half):
            inv = jnp.where(lane < NSA_HDIM, 1.0 / l_sc[2 * pr], 1.0 / l_sc[2 * pr + 1])
            o_ref[:, pr * LANE:(pr + 1) * LANE] = acc_sc[pr] * inv


def nsa_flash(q, kv, tt, msk, s, mode):
    t = NSA_FLASH_T
    nq = s // t
    n_back = WINDOW // t
    half = NSA_GROUP // 2
    if mode == "sel":
        nk = nq
        kmap = lambda g, i, j: (jnp.minimum(i, j), 0)
    else:
        nk = n_back + 1
        kmap = lambda g, i, j: (jnp.maximum(i - n_back + j, 0), 0)
    ins = [q, kv, tt]
    specs = [pl.BlockSpec((t, half * LANE), lambda g, i, j: (i, g)),
             pl.BlockSpec((t, NSA_ROW), kmap),
             pl.BlockSpec((NSA_GROUP, 2, t, t), lambda g, i, j: (g, 0, 0, 0))]
    if mode == "sel":
        ins.append(msk)
        specs.append(pl.BlockSpec((1, t, msk.shape[2]), lambda g, i, j: (g, i, 0)))
    return pl.pallas_call(
        functools.partial(_nsa_flash_kernel, mode=mode, t=t, n_back=n_back),
        grid=(NSA_KV_HEADS, nq, nk), in_specs=specs,
        out_specs=pl.BlockSpec((t, half * LANE), lambda g, i, j: (i, g)),
        out_shape=jax.ShapeDtypeStruct((s, NSA_NQ), F32),
        scratch_shapes=[pltpu.VMEM((NSA_GROUP, t, 1), F32), pltpu.VMEM((NSA_GROUP, t, 1), F32),
                        pltpu.VMEM((half, t, LANE), F32)],
        compiler_params=_cp("parallel", "parallel", "arbitrary"), name="nsa_flash_" + mode)(*ins)


def _nsa_dec_kernel(pt_ref, q_ref, msk_ref, ex_ref, new_ref, st_ref, wnew_ref, dl_ref, dn_ref, dw_ref,
                    *rest, n_pg, ds):
    pages = rest[:n_pg]
    os_ref, ow_ref, m_sc, l_sc, acc_sc = rest[n_pg:]
    j = pl.program_id(1)
    last = pl.num_programs(1) - 1
    nr = NSA_GROUP * DEC_QPAD
    nk = n_pg * PAGE_SIZE

    @pl.when(j == 0)
    def _():
        m_sc[...] = jnp.full(m_sc.shape, NEG_INF, F32)
        l_sc[...] = jnp.zeros(l_sc.shape, F32)
        acc_sc[...] = jnp.zeros(acc_sc.shape, F32)

    kb = [p[0].astype(BF16) for p in pages]
    is_last = jnp.where(j == last, 1.0, 0.0)
    for g in range(NSA_KV_HEADS):
        q = q_ref[0, g]
        s = jnp.concatenate([_bdot_nt(q, k[:, :LANE]) for k in kb], axis=1)
        near = jnp.concatenate([jnp.zeros((nr, nk - PAGE_SIZE), F32), dl_ref[g]], axis=1)
        s = s + near * is_last + (_bdot(msk_ref[0, g, 0], ex_ref[...]) - 1.0) * (-NEG_INF)
        m_prev = m_sc[g]
        m_new = jnp.maximum(m_prev, jnp.max(s, axis=-1, keepdims=True))
        alpha = jnp.exp(m_prev - m_new)
        p = jnp.exp(s - m_new)
        l_sc[g] = alpha * l_sc[g] + jnp.sum(p, axis=-1, keepdims=True)
        pv = _bdot(p[:, :PAGE_SIZE].astype(BF16), kb[0][:, LANE:])
        for i in range(1, n_pg):
            pv = pv + _bdot(p[:, i * PAGE_SIZE:(i + 1) * PAGE_SIZE].astype(BF16), kb[i][:, LANE:])
        acc_sc[g] = alpha * acc_sc[g] + pv
        m_sc[g] = m_new

    @pl.when(j == last)
    def _():
        kn = new_ref[0].astype(BF16)
        ks = st_ref[0].astype(BF16)
        kw = wnew_ref[0].astype(BF16)
        wn = ks.shape[0]
        qidx = lax.broadcasted_iota(jnp.int32, (nr, DEC_QPAD), 0) % DEC_QPAD
        kidx = lax.broadcasted_iota(jnp.int32, (nr, DEC_QPAD), 1)
        ok_new = (kidx <= qidx) & (kidx < ds)
        wq = lax.broadcasted_iota(jnp.int32, (nr, wn), 0) % DEC_QPAD
        wk = lax.broadcasted_iota(jnp.int32, (nr, wn), 1)
        ok_win = wk > wq
        for g in range(NSA_KV_HEADS):
            q = q_ref[0, g]
            sn = jnp.where(ok_new, _bdot_nt(q, kn[:, :LANE]) + dn_ref[g], NEG_INF)
            m_prev = m_sc[g]
            m_new = jnp.maximum(m_prev, jnp.max(sn, axis=-1, keepdims=True))
            alpha = jnp.exp(m_prev - m_new)
            pn = jnp.where(ok_new, jnp.exp(sn - m_new), 0.0)
            l = alpha * l_sc[g] + jnp.sum(pn, axis=-1, keepdims=True)
            os_ref[0, g] = (alpha * acc_sc[g] + _bdot(pn.astype(BF16), kn[:, LANE:])) / l
            sw = jnp.where(ok_win, _bdot_nt(q, ks[:, :LANE]) + dw_ref[g], NEG_INF)
            sx = jnp.where(ok_new, _bdot_nt(q, kw[:, :LANE]) + dn_ref[g], NEG_INF)
            mw = jnp.maximum(jnp.max(sw, axis=-1, keepdims=True), jnp.max(sx, axis=-1, keepdims=True))
            pw = jnp.where(ok_win, jnp.exp(sw - mw), 0.0)
            px = jnp.where(ok_new, jnp.exp(sx - mw), 0.0)
            lw = jnp.sum(pw, axis=-1, keepdims=True) + jnp.sum(px, axis=-1, keepdims=True)
            ow_ref[0, g] = (_bdot(pw.astype(BF16), ks[:, LANE:]) + _bdot(px.astype(BF16), kw[:, LANE:])) / lw


def nsa_dec(qg, msk, ex, new_rows, state, win_new, d_last, d_new, d_win, cache, page_table, ds):
    b, n_pages = page_table.shape
    n_pg = min(DEC_PAGES, n_pages)
    nr = NSA_GROUP * DEC_QPAD
    g = NSA_KV_HEADS
    fix = lambda a: pl.BlockSpec(a.shape, lambda bi, j, pt: (0,) * a.ndim)
    per_b = lambda a: pl.BlockSpec((1,) + a.shape[1:], lambda bi, j, pt: (bi,) + (0,) * (a.ndim - 1))
    page_specs = [pl.BlockSpec((1, PAGE_SIZE, NSA_ROW), functools.partial(
        lambda bi, j, pt, i: (pt[bi, j * n_pg + i], 0, 0), i=i)) for i in range(n_pg)]
    gs = pltpu.PrefetchScalarGridSpec(
        num_scalar_prefetch=1, grid=(b, n_pages // n_pg),
        in_specs=[per_b(qg),
                  pl.BlockSpec((1, g, 1) + msk.shape[3:], lambda bi, j, pt: (bi, 0, j, 0, 0)),
                  fix(ex), per_b(new_rows), per_b(state), per_b(win_new), fix(d_last), fix(d_new), fix(d_win)]
        + page_specs,
        out_specs=[pl.BlockSpec((1, g, nr, LANE), lambda bi, j, pt: (bi, 0, 0, 0)),
                   pl.BlockSpec((1, g, nr, LANE), lambda bi, j, pt: (bi, 0, 0, 0))],
        scratch_shapes=[pltpu.VMEM((g, nr, 1), F32), pltpu.VMEM((g, nr, 1), F32), pltpu.VMEM((g, nr, LANE), F32)])
    return pl.pallas_call(
        functools.partial(_nsa_dec_kernel, n_pg=n_pg, ds=ds), grid_spec=gs,
        out_shape=[jax.ShapeDtypeStruct((b, g, nr, LANE), F32), jax.ShapeDtypeStruct((b, g, nr, LANE), F32)],
        compiler_params=_cp("parallel", "arbitrary"), name="nsa_dec")(
            page_table, qg, msk, ex, new_rows, state, win_new, d_last, d_new, d_win, *([cache] * n_pg))


def _nsa_out_kernel(oc_ref, os_ref, ow_ref, g_ref, eg_ref, w_ref, y_ref):
    gh, gl = _split(g_ref[...])
    ge = _bdot(gh, eg_ref[...]) + _bdot(gl, eg_ref[...])
    o = (oc_ref[...] * ge[:, :NSA_NQ] + os_ref[...] * ge[:, NSA_NQ:2 * NSA_NQ]
         + ow_ref[...] * ge[:, 2 * NSA_NQ:])
    y_ref[...] = _dot(o, w_ref[...])


def nsa_out(o_cmp, o_slc, o_win, gate, eg, w_o, tm=512):
    m = o_cmp.shape[0]
    tm = _tile(m, tm)
    d = w_o.shape[1]
    tile = lambda a: pl.BlockSpec((tm, a.shape[1]), lambda i: (i, 0))
    full = lambda a: pl.BlockSpec(a.shape, lambda i: (0, 0))
    return pl.pallas_call(
        _nsa_out_kernel, grid=(m // tm,),
        in_specs=[tile(o_cmp), tile(o_slc), tile(o_win), tile(gate), full(eg), full(w_o)],
        out_specs=pl.BlockSpec((tm, d), lambda i: (i, 0)),
        out_shape=jax.ShapeDtypeStruct((m, d), F32),
        compiler_params=_cp("parallel"), name="nsa_out")(o_cmp, o_slc, o_win, gate, eg, w_o)


def _t5_bucket(dist):
    max_exact = REL_BUCKETS // 2
    n = jnp.maximum(dist, 0)
    nf = jnp.maximum(n, max_exact).astype(F32)
    large = max_exact + (jnp.log(nf / max_exact) / math.log(REL_MAX_DIST / max_exact)
                         * (REL_BUCKETS - max_exact)).astype(jnp.int32)
    return jnp.where(n < max_exact, n, jnp.minimum(large, REL_BUCKETS - 1))


def _nsa_constants(n_chunks, n_sel):
    mm_ = np.zeros((n_chunks, n_sel), np.float32)
    r = SEL_BLOCK // CMP_STRIDE
    for jb in range(n_sel):
        for a, wgt in enumerate(SEL_OVERLAP_W):
            n = r * jb - 1 + a
            if 0 <= n < n_chunks:
                mm_[n, jb] = wgt
    eg = np.zeros((LANE, N_BRANCH * NSA_NQ), np.float32)
    for h in range(NSA_HEADS):
        for br in range(N_BRANCH):
            eg[h * N_BRANCH + br, br * NSA_NQ + h * NSA_HDIM:br * NSA_NQ + (h + 1) * NSA_HDIM] = 1.0
    return jnp.asarray(mm_, BF16), jnp.asarray(eg, BF16)


def _nsa_weights(w_in, pe_k, w_ck, pe_v, w_cv):
    d = w_in.shape[0]
    n_main = NSA_NQ + N_BRANCH * NSA_ROW
    w_main = w_in[:, :n_main]
    w_gate = jnp.pad(w_in[:, n_main:], ((0, 0), (0, LANE - NSA_HEADS * N_BRANCH)))
    half = CMP_LEN // 2
    w = jnp.stack([w_ck, w_cv]).reshape(2, 2, half, NSA_HDIM, NSA_HDIM)
    eye = jnp.eye(2, dtype=F32)
    w_big = jnp.einsum("khlde,kK,gG->lkgdhKGe", w, eye, eye).reshape(half * NSA_ROW, 2 * NSA_ROW)
    pe = jnp.stack([pe_k, pe_v]).reshape(2, 2, half, NSA_HDIM)
    pe_rows = jnp.broadcast_to(jnp.transpose(pe, (1, 2, 0, 3))[:, :, :, None, :],
                               (2, half, 2, NSA_KV_HEADS, NSA_HDIM)).reshape(2, half * NSA_ROW)
    pe_rows = jnp.pad(pe_rows, ((0, 6), (0, 0)))
    return w_main, w_gate, w_big.astype(BF16), pe_rows


def _bias_tables(rel_bias, ds, w_buf):
    nd = 2 * REL_MAX_DIST
    onehot = (_t5_bucket(jnp.arange(nd))[None, :] == jnp.arange(LANE)[:, None]).astype(F32)
    tab_t = jnp.pad(rel_bias.T, ((0, 0), (0, LANE - REL_BUCKETS)))
    bd = mm(tab_t, onehot, hi=True)
    delta = bd[:, :REL_MAX_DIST + 1] - bd[:, REL_MAX_DIST:REL_MAX_DIST + 1]

    def look(dist):
        dist = np.asarray(dist)
        val = delta[:, np.clip(dist, 0, REL_MAX_DIST).reshape(-1)].reshape((NSA_HEADS,) + dist.shape)
        return jnp.where(jnp.asarray(dist >= 0)[None], val, 0.0)

    i = np.arange(NSA_CMP_TQ)[:, None]
    mrel = np.arange(CMP_WIN)[None, :] - CMP_WIN // 2
    nb = look(i - CMP_STRIDE * mrel - (CMP_LEN - 1))
    nb_hi = nb.astype(BF16)
    nb_lo = (nb - nb_hi.astype(F32)).astype(BF16)
    nb_tab = jnp.concatenate([nb_hi, nb_lo], axis=-1)
    t = NSA_FLASH_T
    ii = np.arange(t)[:, None] - np.arange(t)[None, :]
    tt = jnp.stack([look(ii), look(t + ii)], axis=1)

    def per_group(x):
        return x.reshape(NSA_KV_HEADS, NSA_GROUP * DEC_QPAD, x.shape[-1])

    qi = np.arange(DEC_QPAD)[:, None]
    d_last = per_group(look(PAGE_SIZE + qi - np.arange(PAGE_SIZE)[None, :]))
    d_new = per_group(look(qi - np.arange(DEC_QPAD)[None, :]))
    d_win = per_group(look(w_buf + qi - np.arange(w_buf)[None, :]))
    return nb_tab, tt, d_last, d_new, d_win


def nsa_layer(xp, xs, mod_p, mod_s, cache_cmp, cache_slc, state_win, page_table, ds,
              w_in, pe_k, w_ck, pe_v, w_cv, w_o, rel_bias):
    s = xp.shape[0]
    b, n_pages = page_table.shape
    past = n_pages * PAGE_SIZE
    w_buf = state_win.shape[1]
    assert s % NSA_FLASH_T == 0 and past % PAGE_SIZE == 0 and ds < CMP_STRIDE and w_buf == WINDOW
    n_chunks = s // CMP_STRIDE
    n_sel = s // SEL_BLOCK
    w_main, w_gate, w_big, pe_rows = _nsa_weights(w_in, pe_k, w_ck, pe_v, w_cv)
    nb_tab, tt, d_last, d_new, d_win = _bias_tables(rel_bias, ds, w_buf)
    pe_out = mm(pe_rows, w_big)
    cbias = jnp.concatenate([pe_out[0, :NSA_ROW] + pe_out[1, NSA_ROW:], jnp.zeros((NSA_ROW,), F32)])[None]

    q_p, kvc_p, kvs_p, kvw_p, gate_p = nsa_in(xp, mod_p, w_main, w_gate)
    ab_p = mm(kvc_p.reshape(n_chunks, CMP_STRIDE * NSA_ROW), w_big, bias=cbias)[None]
    m_p, eg = _nsa_constants(n_chunks, n_sel)
    oc_p, imp_p = nsa_cmp(q_p, ab_p, nb_tab, m_p, NSA_CMP_TQ, 0)
    msk_p = sel_topk(imp_p, NSA_FLASH_T, 0, s, min(SEL_TOPK, n_sel))
    os_p = nsa_flash(q_p, kvs_p.astype(BF16), tt, msk_p, s, "sel")
    ow_p = nsa_flash(q_p, kvw_p.astype(BF16), tt, None, s, "win")
    y_p = nsa_out(oc_p, os_p, ow_p, gate_p, eg, w_o)

    q_s, kvc_s, kvs_s, kvw_s, gate_s = nsa_in(xs, mod_s, w_main, w_gate)
    nc_s = past // CMP_STRIDE
    ns_s = past // SEL_BLOCK
    cpp = PAGE_SIZE // CMP_STRIDE
    ab_s = cmp_paged(cache_cmp.reshape(-1, cpp, CMP_STRIDE * NSA_ROW), page_table, w_big, cbias)
    m_s, _ = _nsa_constants(nc_s, ns_s)
    pad_q = lambda a: jnp.pad(a.reshape(b, ds, -1), ((0, 0), (0, DEC_QPAD - ds), (0, 0)))
    q_s8 = pad_q(q_s)
    oc_s, imp_s = nsa_cmp(q_s8.reshape(b * DEC_QPAD, NSA_NQ), ab_s, nb_tab, m_s, DEC_QPAD, past)
    k_past = min(SEL_TOPK, ns_s + 1) - 1
    msk_s = sel_topk(imp_s, DEC_QPAD * 32, past, DEC_QPAD, k_past)
    n_pg = min(DEC_PAGES, n_pages)
    bpc = n_pg * PAGE_SIZE // SEL_BLOCK
    msk_d = msk_s.reshape(NSA_KV_HEADS, b, 1, DEC_QPAD, ns_s // bpc, bpc)
    msk_d = jnp.broadcast_to(msk_d, (NSA_KV_HEADS, b, NSA_GROUP, DEC_QPAD, ns_s // bpc, bpc))
    msk_d = jnp.transpose(msk_d, (1, 0, 4, 2, 3, 5)).reshape(b, NSA_KV_HEADS, ns_s // bpc,
                                                              NSA_GROUP * DEC_QPAD, bpc)
    ex = jnp.asarray(np.arange(n_pg * PAGE_SIZE)[None, :] // SEL_BLOCK == np.arange(bpc)[:, None], BF16)
    qg = q_s8.reshape(b, DEC_QPAD, NSA_KV_HEADS, NSA_GROUP, NSA_HDIM)
    qg = jnp.transpose(qg, (0, 2, 3, 1, 4)).reshape(b, NSA_KV_HEADS, NSA_GROUP * DEC_QPAD, NSA_HDIM)
    qg = jnp.stack([jnp.pad(qg[:, 0], ((0, 0), (0, 0), (0, NSA_HDIM))),
                    jnp.pad(qg[:, 1], ((0, 0), (0, 0), (NSA_HDIM, 0)))], axis=1)
    os_s, ow_s = nsa_dec(qg, msk_d, ex, pad_q(kvs_s), state_win.reshape(b, w_buf, NSA_ROW), pad_q(kvw_s),
                         d_last, d_new, d_win, cache_slc.reshape(-1, PAGE_SIZE, NSA_ROW), page_table, ds)

    def ungroup(o):
        o = jnp.stack([o[:, 0, :, :NSA_HDIM], o[:, 1, :, NSA_HDIM:]], axis=1)
        o = o.reshape(b, NSA_KV_HEADS, NSA_GROUP, DEC_QPAD, NSA_HDIM)
        return jnp.transpose(o, (0, 3, 1, 2, 4))[:, :ds].reshape(b * ds, NSA_NQ)

    oc_s = oc_s.reshape(b, DEC_QPAD, NSA_NQ)[:, :ds].reshape(b * ds, NSA_NQ)
    y_s = nsa_out(oc_s, ungroup(os_s), ungroup(ow_s), gate_s, eg, w_o)
    return y_p, y_s, (kvc_p, kvs_p, kvw_p), (kvc_s, kvs_s, kvw_s)


def kernel(x_prompt, x_sample, cache_mla, cache_nsa_cmp, cache_nsa_slc, state_nsa_win, page_table,
           c_prompt, c_sample, mla_w_in, mla_g_q, mla_g_kv, mla_w_uq, mla_w_uk, mla_w_uv, mla_w_o,
           nsa_w_in, nsa_pe_k, nsa_w_ck, nsa_pe_v, nsa_w_cv, nsa_w_o, rel_bias, ada_w, ada_b, ln_g, ln_b,
           moe_w_router, moe_b_router, moe_w_gate, moe_w_up, moe_w_down, moe_ws_gate, moe_ws_up, moe_ws_down):
    bp, s, d = x_prompt.shape
    b, ds, _ = x_sample.shape
    assert bp == 1, "one prompt sequence"
    xp = x_prompt.reshape(s, d)
    xs = x_sample.reshape(b * ds, d)
    c_all = jnp.concatenate([c_prompt, c_sample], axis=0)
    c_all = jnp.pad(c_all, ((0, -(1 + b) % 16), (0, 0)))
    g, hd = NSA_KV_HEADS, NSA_HDIM
    outs = {}
    for l in range(DEPTH):
        jl = l // 2
        m_all = mm(c_all, ada_w[l], bias=ada_b[l][None], act_in="silu", tn=2 * d)
        mp = [m_all[0:1, k * d:(k + 1) * d] for k in range(6)]
        ms = [jnp.repeat(m_all[1:1 + b, k * d:(k + 1) * d], ds, axis=0) for k in range(6)]
        if l % 2 == 0:
            y_p, y_s, rows_p, rows_s = mla_layer(
                xp, xs, (mp[1], mp[0]), (ms[1], ms[0]), cache_mla[jl], page_table, ds,
                mla_w_in[jl], mla_g_q[jl], mla_g_kv[jl], mla_w_uq[jl], mla_w_uk[jl], mla_w_uv[jl], mla_w_o[jl])
            outs["mla_p"] = rows_p.reshape(1, 1, s, MLA_C)
            outs["mla_s"] = rows_s.reshape(1, b, ds, MLA_C)
        else:
            y_p, y_s, kv_p, kv_s = nsa_layer(
                xp, xs, (mp[1], mp[0]), (ms[1], ms[0]), cache_nsa_cmp[jl], cache_nsa_slc[jl], state_nsa_win[jl],
                page_table, ds, nsa_w_in[jl], nsa_pe_k[jl], nsa_w_ck[jl], nsa_pe_v[jl], nsa_w_cv[jl],
                nsa_w_o[jl], rel_bias)
            n_keep = min(WINDOW, s)
            outs["cmp_p"] = kv_p[0].reshape(1, 1, s, 2, g, hd)
            outs["slc_p"] = kv_p[1].reshape(1, 1, s, 2, g, hd)
            outs["win_p"] = kv_p[2][s - n_keep:].reshape(1, 1, n_keep, 2, g, hd)
            outs["cmp_s"] = kv_s[0].reshape(1, b, ds, 2, g, hd)
            outs["slc_s"] = kv_s[1].reshape(1, b, ds, 2, g, hd)
            w_buf = state_nsa_win.shape[2]
            win_cat = jnp.concatenate([state_nsa_win[jl], kv_s[2].reshape(b, ds, 2, g, hd)], axis=1)
            outs["win_s"] = win_cat[:, -w_buf:][None]
        lg = lambda k: ln_g[l, k][None]
        lb = lambda k: ln_b[l, k][None]
        xp, hp = ln_res(xp, [y_p], mp[2], lg(0), lb(0), nxt=(mp[4], mp[3]))
        xs, hs = ln_res(xs, [y_s], ms[2], lg(0), lb(0), nxt=(ms[4], ms[3]))
        routed, shared = moe_ffn(jnp.concatenate([hp, hs], axis=0), moe_w_router[l], moe_b_router[l],
                                 moe_w_gate[l], moe_w_up[l], moe_w_down[l],
                                 moe_ws_gate[l], moe_ws_up[l], moe_ws_down[l])
        xp = ln_res(xp, [routed[:s], shared[:s]], mp[5], lg(1), lb(1))
        xs = ln_res(xs, [routed[s:], shared[s:]], ms[5], lg(1), lb(1))
    return (xp.reshape(1, s, d), xs.reshape(b, ds, d), outs["mla_p"], outs["cmp_p"], outs["slc_p"], outs["win_p"],
            outs["mla_s"], outs["cmp_s"], outs["slc_s"], outs["win_s"])
```
